```python
import jax, jax.numpy as jnp
from jax import lax
import numpy as np

D_MODEL = 4096
BATCH = 2
SEQ = 4096
DEPTH = 1
DEC_BATCH = 32
DEC_SEQ = 1
PAST_LEN = 8192
PAGE_SIZE = 128

MIX_WIDTH = D_MODEL
W_A = MIX_WIDTH // 2
W_B = MIX_WIDTH - W_A
HEAD_DIM = 128
N_GROUPS_A = W_A // HEAD_DIM
N_HEADS_B = W_B // HEAD_DIM
IN_WIDTH = 3 * W_A + 4 * W_B
CHUNK = 128
Q_BLOCK = 128
SB_LOGIT_BIAS = 10.0
EPS = 1e-6

kernel_name = 'hybrid_gmlp_stickbreaking_decode_step'


def _rms_norm(x, g):
    xf = x.astype(jnp.float32)
    y = xf * lax.rsqrt(jnp.mean(xf * xf, axis=-1, keepdims=True) + EPS)
    return (y * g.astype(jnp.float32)).astype(x.dtype)


def _layer_norm(x, g, b):
    xf = x.astype(jnp.float32)
    mu = jnp.mean(xf, axis=-1, keepdims=True)
    xc = xf - mu
    y = xc * lax.rsqrt(jnp.mean(xc * xc, axis=-1, keepdims=True) + EPS)
    return (y * g.astype(jnp.float32) + b.astype(jnp.float32)).astype(x.dtype)


def _input_projection(x, norm_pre, w_in, ln_v_g, ln_v_b):
    bsz, n = x.shape[0], x.shape[1]
    h = _rms_norm(x, norm_pre)
    z = jnp.einsum('bsd,de->bse', h, w_in)
    cuts = [W_A, 2 * W_A, 3 * W_A, 3 * W_A + W_B, 3 * W_A + 2 * W_B, 3 * W_A + 3 * W_B]
    u_a, v_a, g_a, q_b, k_b, v_b, g_b = jnp.split(z, cuts, axis=-1)
    v_a = _layer_norm(v_a, ln_v_g, ln_v_b)
    heads = lambda t: t.reshape(bsz, n, N_HEADS_B, HEAD_DIM)
    return u_a, v_a, g_a, heads(q_b), heads(k_b), heads(v_b), g_b


def _spatial_gate(u_a, v_a, w_spatial, b_spatial):
    bsz, n, _ = v_a.shape
    c = min(n, CHUNK)
    nc = n // c
    causal = jnp.tril(jnp.ones((c, c), dtype=bool))
    w = jnp.where(causal[None], w_spatial[:, :c, :c], 0.0).astype(v_a.dtype)
    vg = v_a.reshape(bsz, nc, c, N_GROUPS_A, HEAD_DIM)
    bias = b_spatial[:, :c].T.astype(v_a.dtype)[None, None, :, :, None]
    mix = jnp.einsum('gts,bcsgd->bctgd', w, vg) + bias
    return u_a * mix.reshape(bsz, n, W_A)


def _stick_breaking(q, k, v, q_pos, k_pos, b_sb):
    z = jnp.einsum('bqhd,bkhd->bhqk', q.astype(jnp.float32), k.astype(jnp.float32)) * (HEAD_DIM ** -0.5)
    z = z + b_sb.astype(jnp.float32)[None, :, None, None]
    causal = (k_pos[None, :] < q_pos[:, None])[None, None]
    log_keep = jnp.where(causal, jax.nn.log_sigmoid(-z), 0.0)
    between = lax.cumsum(log_keep, axis=3, reverse=True) - log_keep
    weights = jnp.where(causal, jnp.exp(jax.nn.log_sigmoid(z) + between), 0.0)
    out = jnp.einsum('bhqk,bkhd->bqhd', weights, v.astype(jnp.float32))
    return out.astype(q.dtype)


def _prompt_stick_breaking(q, k, v, b_sb):
    bsz, n = q.shape[0], q.shape[1]
    nb = n // Q_BLOCK
    k_pos = jnp.arange(n)
    q_blocks = q.reshape(bsz, nb, Q_BLOCK, N_HEADS_B, HEAD_DIM).transpose(1, 0, 2, 3, 4)
    pos_blocks = k_pos.reshape(nb, Q_BLOCK)
    out = lax.map(lambda a: _stick_breaking(a[0], k, v, a[1], k_pos, b_sb), (q_blocks, pos_blocks))
    return out.transpose(1, 0, 2, 3, 4).reshape(bsz, n, W_B)


def _output(x, a_out, g_a, b_out, g_b, norm_br_a, norm_br_b, w_out, norm_post):
    ya = _rms_norm(a_out, norm_br_a) * jax.nn.silu(g_a)
    yb = _rms_norm(b_out, norm_br_b) * jax.nn.silu(g_b)
    o = jnp.einsum('bse,ed->bsd', jnp.concatenate([ya, yb], axis=-1), w_out)
    return x + _rms_norm(o, norm_post)


def setup_inputs(seed: int = 0) -> dict:
    key = jax.random.key(seed)
    ks = jax.random.split(key, 16)
    f32 = jnp.float32
    n_pages = PAST_LEN // PAGE_SIZE
    n_used = DEC_BATCH * n_pages
    n_phys = n_used + max(1, n_used // 4)
    page_table = jax.random.permutation(ks[4], n_phys)[:n_used].reshape(DEC_BATCH, n_pages).astype(jnp.int32)
    cache_shape = (DEPTH, n_phys, PAGE_SIZE, N_HEADS_B, HEAD_DIM)
    return {
        'x_prompt': jax.random.normal(ks[0], (BATCH, SEQ, D_MODEL), f32),
        'x_sample': jax.random.normal(ks[1], (DEC_BATCH, DEC_SEQ, D_MODEL), f32),
        'cache_k': jax.random.normal(ks[2], cache_shape, f32),
        'cache_v': jax.random.normal(ks[3], cache_shape, f32),
        'page_table': page_table,
        'norm_pre': 1.0 + 0.05 * jax.random.normal(ks[5], (DEPTH, D_MODEL), f32),
        'w_in': jax.random.normal(ks[6], (DEPTH, D_MODEL, IN_WIDTH), f32) * (D_MODEL ** -0.5),
        'ln_v_g': 1.0 + 0.05 * jax.random.normal(ks[7], (DEPTH, W_A), f32),
        'ln_v_b': 0.02 * jax.random.normal(ks[8], (DEPTH, W_A), f32),
        'w_spatial': jax.random.normal(ks[9], (DEPTH, N_GROUPS_A, CHUNK, CHUNK), f32) * (CHUNK ** -0.5),
        'b_spatial': 1.0 + 0.1 * jax.random.normal(ks[10], (DEPTH, N_GROUPS_A, CHUNK), f32),
        'b_sb': -SB_LOGIT_BIAS + 0.1 * jax.random.normal(ks[15], (DEPTH, N_HEADS_B), f32),
        'norm_br_a': 1.0 + 0.05 * jax.random.normal(ks[11], (DEPTH, W_A), f32),
        'norm_br_b': 1.0 + 0.05 * jax.random.normal(ks[12], (DEPTH, W_B), f32),
        'w_out': jax.random.normal(ks[13], (DEPTH, MIX_WIDTH, D_MODEL), f32) * (MIX_WIDTH ** -0.5),
        'norm_post': 1.0 + 0.05 * jax.random.normal(ks[14], (DEPTH, D_MODEL), f32),
    }


def reference(x_prompt, x_sample, cache_k, cache_v, page_table, norm_pre, w_in, ln_v_g, ln_v_b,
              w_spatial, b_spatial, b_sb, norm_br_a, norm_br_b, w_out, norm_post):
    hp, hs = x_prompt, x_sample
    dec_b, n_new = x_sample.shape[0], x_sample.shape[1]
    past_len = page_table.shape[1] * cache_k.shape[2]
    kp_rows, vp_rows, ks_rows, vs_rows, cp_rows, cs_rows = [], [], [], [], [], []
    for l in range(DEPTH):
        u_a, v_a, g_a, q_b, k_b, v_b, g_b = _input_projection(hp, norm_pre[l], w_in[l], ln_v_g[l], ln_v_b[l])
        a_out = _spatial_gate(u_a, v_a, w_spatial[l], b_spatial[l])
        b_out = _prompt_stick_breaking(q_b, k_b, v_b, b_sb[l])
        new_hp = _output(hp, a_out, g_a, b_out, g_b, norm_br_a[l], norm_br_b[l], w_out[l], norm_post[l])
        kp_rows.append(k_b)
        vp_rows.append(v_b)
        cp_rows.append(v_a[:, -CHUNK:])

        su_a, sv_a, sg_a, sq_b, sk_b, sv_b, sg_b = _input_projection(hs, norm_pre[l], w_in[l], ln_v_g[l], ln_v_b[l])
        sa_out = _spatial_gate(su_a, sv_a, w_spatial[l], b_spatial[l])
        k_past = cache_k[l][page_table].reshape(dec_b, past_len, N_HEADS_B, HEAD_DIM)
        v_past = cache_v[l][page_table].reshape(dec_b, past_len, N_HEADS_B, HEAD_DIM)
        k_all = jnp.concatenate([k_past, sk_b.astype(k_past.dtype)], axis=1)
        v_all = jnp.concatenate([v_past, sv_b.astype(v_past.dtype)], axis=1)
        k_pos = jnp.arange(past_len + n_new)
        q_pos = past_len + jnp.arange(n_new)
        sb_out = _stick_breaking(sq_b, k_all, v_all, q_pos, k_pos, b_sb[l]).reshape(dec_b, n_new, W_B)
        new_hs = _output(hs, sa_out, sg_a, sb_out, sg_b, norm_br_a[l], norm_br_b[l], w_out[l], norm_post[l])
        ks_rows.append(sk_b)
        vs_rows.append(sv_b)
        cs_rows.append(sv_a)
        hp, hs = new_hp, new_hs
    return (hp, hs, jnp.stack(kp_rows), jnp.stack(vp_rows), jnp.stack(ks_rows), jnp.stack(vs_rows),
            jnp.stack(cp_rows), jnp.stack(cs_rows))
```

```python
import functools
import math

import jax
import jax.numpy as jnp
from jax import lax
from jax.experimental import pallas as pl
from jax.experimental.pallas import tpu as pltpu

F32 = jnp.float32
BF16 = jnp.bfloat16

HEAD_DIM = 128
CHUNK = 128
EPS = 1e-6
N_SECTIONS = 7
LOG2E = math.log2(math.e)
LANES = 128
VMEM_LIMIT = 56 * 1024 * 1024


def _cparams(*sem):
    return pltpu.CompilerParams(dimension_semantics=sem, vmem_limit_bytes=VMEM_LIMIT)


def _silu(g):
    return g * (1.0 / (1.0 + jnp.exp(-g)))


def _in_proj_kernel(x_ref, g_ref, w_ref, u_o, va_o, ga_o, q_o, k_o, v_o, kb_o, vb_o, gb_o, h_scr, *, q_scale):
    @pl.when(pl.program_id(1) == 0)
    def _():
        x = x_ref[...]
        ms = jnp.mean(x * x, axis=-1, keepdims=True)
        h_scr[...] = (x * lax.rsqrt(ms + EPS) * g_ref[...]).astype(BF16)

    z = jnp.dot(h_scr[...], w_ref[...], preferred_element_type=F32)
    sec = lambda s: z[:, s * HEAD_DIM:(s + 1) * HEAD_DIM]
    u_o[...] = sec(0).astype(BF16)
    va_o[...] = sec(1).astype(BF16)
    ga_o[...] = sec(2).astype(BF16)
    q_o[...] = (sec(3) * q_scale).astype(BF16)
    k_o[...] = sec(4)
    v_o[...] = sec(5)
    kb_o[...] = sec(4).astype(BF16)
    vb_o[...] = sec(5).astype(BF16)
    gb_o[...] = sec(6).astype(BF16)


def _in_proj(x2d, norm_pre, w_perm, tm):
    m, d = x2d.shape
    width = w_perm.shape[1] // N_SECTIONS
    nj = width // HEAD_DIM
    q_scale = (HEAD_DIM ** -0.5) * LOG2E
    blk = pl.BlockSpec((tm, HEAD_DIM), lambda i, j: (i, j))
    bf = jax.ShapeDtypeStruct((m, width), BF16)
    f32 = jax.ShapeDtypeStruct((m, width), F32)
    return pl.pallas_call(
        functools.partial(_in_proj_kernel, q_scale=q_scale),
        grid=(m // tm, nj),
        in_specs=[
            pl.BlockSpec((tm, d), lambda i, j: (i, 0)),
            pl.BlockSpec((1, d), lambda i, j: (0, 0)),
            pl.BlockSpec((d, N_SECTIONS * HEAD_DIM), lambda i, j: (0, j)),
        ],
        out_specs=[blk] * 9,
        out_shape=[bf, bf, bf, bf, f32, f32, bf, bf, bf],
        scratch_shapes=[pltpu.VMEM((tm, d), BF16)],
        compiler_params=_cparams("parallel", "arbitrary"),
        name="in_proj",
    )(x2d, norm_pre.reshape(1, d), w_perm)


def _layer_norm(va, g, b):
    mu = jnp.mean(va, axis=-1, keepdims=True)
    xc = va - mu
    var = jnp.mean(xc * xc, axis=-1, keepdims=True)
    return xc * lax.rsqrt(var + EPS) * g + b


def _branch_a_kernel(u_ref, va_ref, ga_ref, lng_ref, lnb_ref, w_ref, bias_ref, nbr_ref, ya_ref, cv_ref):
    vn = _layer_norm(va_ref[...].astype(F32), lng_ref[...], lnb_ref[...])
    cv_ref[0] = vn
    vnb = vn.astype(BF16)
    n_groups = w_ref.shape[0]
    mixes = [jnp.dot(w_ref[g], vnb[:, g * HEAD_DIM:(g + 1) * HEAD_DIM], preferred_element_type=F32)
             for g in range(n_groups)]
    mix = jnp.concatenate(mixes, axis=1) + bias_ref[...]
    a = u_ref[...].astype(F32) * mix
    ms = jnp.mean(a * a, axis=-1, keepdims=True)
    ya = a * lax.rsqrt(ms + EPS) * nbr_ref[...] * _silu(ga_ref[...].astype(F32))
    ya_ref[...] = ya.astype(BF16)


def _branch_a(u, va, ga, ln_g, ln_b, w_tril, bias_full, norm_br_a, bsz, seq):
    m, wa = u.shape
    nc = seq // CHUNK
    row = pl.BlockSpec((CHUNK, wa), lambda b, c: (b * nc + c, 0))
    vec = pl.BlockSpec((1, wa), lambda b, c: (0, 0))
    return pl.pallas_call(
        _branch_a_kernel,
        grid=(bsz, nc),
        in_specs=[row, row, row, vec, vec,
                  pl.BlockSpec(w_tril.shape, lambda b, c: (0, 0, 0)),
                  pl.BlockSpec((CHUNK, wa), lambda b, c: (0, 0)),
                  vec],
        out_specs=[row, pl.BlockSpec((1, CHUNK, wa), lambda b, c: (b, 0, 0))],
        out_shape=[jax.ShapeDtypeStruct((m, wa), BF16), jax.ShapeDtypeStruct((bsz, CHUNK, wa), F32)],
        compiler_params=_cparams("parallel", "arbitrary"),
        name="branch_a",
    )(u, va, ga, ln_g.reshape(1, wa), ln_b.reshape(1, wa), w_tril, bias_full, norm_br_a.reshape(1, wa))


def _branch_a_one_kernel(u_ref, va_ref, ga_ref, lng_ref, lnb_ref, w0_ref, b0_ref, nbr_ref, ya_ref, cv_ref):
    vn = _layer_norm(va_ref[...].astype(F32), lng_ref[...], lnb_ref[...])
    cv_ref[...] = vn
    mix = w0_ref[...] * vn + b0_ref[...]
    a = u_ref[...].astype(F32) * mix
    ms = jnp.mean(a * a, axis=-1, keepdims=True)
    ya = a * lax.rsqrt(ms + EPS) * nbr_ref[...] * _silu(ga_ref[...].astype(F32))
    ya_ref[...] = ya.astype(BF16)


def _branch_a_one(u, va, ga, ln_g, ln_b, w0, b0, norm_br_a):
    m, wa = u.shape
    return pl.pallas_call(
        _branch_a_one_kernel,
        out_shape=[jax.ShapeDtypeStruct((m, wa), BF16), jax.ShapeDtypeStruct((m, wa), F32)],
        compiler_params=pltpu.CompilerParams(vmem_limit_bytes=VMEM_LIMIT),
        name="branch_a_one",
    )(u, va, ga, ln_g.reshape(1, wa), ln_b.reshape(1, wa), w0.reshape(1, wa), b0.reshape(1, wa),
      norm_br_a.reshape(1, wa))


def _softplus2(s2):
    return jnp.maximum(s2, 0.0) + jnp.log(1.0 + jnp.exp2(-jnp.abs(s2))) * LOG2E


def _sb_prompt_kernel(bsb_ref, q_ref, k_ref, v_ref, o_ref, carry_scr, acc_scr, *, tq):
    h = pl.program_id(1)
    i = pl.program_id(2)
    b2 = bsb_ref[h] * LOG2E
    q = q_ref[...]
    row = lax.broadcasted_iota(jnp.int32, (tq, tq), 0)
    col = lax.broadcasted_iota(jnp.int32, (tq, tq), 1)
    later = (row > col).astype(BF16)
    ones = jnp.ones((tq, LANES), BF16)
    reps = tq // LANES

    def block(j, masked):
        start = pl.multiple_of(j * tq, tq)
        kb = k_ref[pl.ds(start, tq), :]
        vb = v_ref[pl.ds(start, tq), :]
        s2 = lax.dot_general(q, kb, (((1,), (1,)), ((), ())), preferred_element_type=F32) + b2
        sp = _softplus2(s2)
        lb = s2 - sp
        if masked:
            causal = col < row
            sp = jnp.where(causal, sp, 0.0)
        spb = sp.astype(BF16)
        between = jnp.dot(spb, later, preferred_element_type=F32)
        carry = carry_scr[...]
        w = jnp.exp2(lb - between - jnp.concatenate([carry] * reps, axis=1))
        if masked:
            w = jnp.where(causal, w, 0.0)
        acc_scr[...] += jnp.dot(w.astype(BF16), vb, preferred_element_type=F32)
        carry_scr[...] = carry + jnp.dot(spb, ones, preferred_element_type=F32)

    carry_scr[...] = jnp.zeros_like(carry_scr)
    acc_scr[...] = jnp.zeros_like(acc_scr)
    block(i, True)

    def body(t, c):
        block(i - 1 - t, False)
        return c

    lax.fori_loop(0, i, body, 0)
    o_ref[...] = acc_scr[...]


def _sb_prompt(q, kb, vb, b_sb, bsz, seq, tq):
    m, wb = q.shape
    n_heads = wb // HEAD_DIM
    nq = seq // tq
    kv = pl.BlockSpec((seq, HEAD_DIM), lambda b, h, i: (b, h))
    return pl.pallas_call(
        functools.partial(_sb_prompt_kernel, tq=tq),
        grid=(bsz, n_heads, nq),
        in_specs=[
            pl.BlockSpec(memory_space=pltpu.SMEM),
            pl.BlockSpec((tq, HEAD_DIM), lambda b, h, i: (b * nq + i, h)),
            kv, kv,
        ],
        out_specs=pl.BlockSpec((tq, HEAD_DIM), lambda b, h, i: (b * nq + i, h)),
        out_shape=jax.ShapeDtypeStruct((m, wb), F32),
        scratch_shapes=[pltpu.VMEM((tq, LANES), F32), pltpu.VMEM((tq, HEAD_DIM), F32)],
        compiler_params=_cparams("parallel", "parallel", "arbitrary"),
        name="sb_prompt",
    )(b_sb, q, kb, vb)


PAGES_PER_STEP = 8


def _sb_decode_kernel(pt_ref, bsb_ref, qx_ref, *refs, n_heads, page):
    npp = PAGES_PER_STEP
    k_refs = refs[:npp]
    v_refs = refs[npp:2 * npp]
    o_ref = refs[2 * npp]
    kb_scr, vb_scr, acc_scr, gcarry_scr = refs[2 * npp + 1:]
    g = pl.program_id(1)
    wb = n_heads * HEAD_DIM

    @pl.when(g == 0)
    def _():
        acc_scr[...] = jnp.zeros_like(acc_scr)
        gcarry_scr[...] = jnp.zeros_like(gcarry_scr)

    for p in range(npp):
        kb_scr[p * page:(p + 1) * page, :] = k_refs[p][0, 0].astype(BF16)
        vb_scr[p * page:(p + 1) * page, :] = v_refs[p][0, 0].astype(BF16)

    r = lax.broadcasted_iota(jnp.int32, (wb, LANES), 0)
    lane = lax.broadcasted_iota(jnp.int32, (wb, LANES), 1)
    qsel = jnp.where(r // HEAD_DIM == lane % n_heads, qx_ref[0], jnp.zeros((), BF16))
    s_full = jnp.dot(kb_scr[...], qsel, preferred_element_type=F32)
    lane_s = lax.broadcasted_iota(jnp.int32, (page, LANES), 1)
    s2 = jnp.zeros((page, LANES), F32)
    for p in range(npp):
        s2 = jnp.where(lane_s // n_heads == p, s_full[p * page:(p + 1) * page, :], s2)
    bias = bsb_ref[...] * LOG2E
    s2 = s2 + bias

    sp = _softplus2(s2)
    lb = s2 - sp
    kr = lax.broadcasted_iota(jnp.int32, (page, page), 0)
    kc = lax.broadcasted_iota(jnp.int32, (page, page), 1)
    later = (kc > kr).astype(BF16)
    between = jnp.dot(later, sp.astype(BF16), preferred_element_type=F32)

    tot = jnp.broadcast_to(jnp.sum(sp, axis=0, keepdims=True), (8, LANES))
    lane8 = lax.broadcasted_iota(jnp.int32, (8, LANES), 1)
    incl = tot
    gsum = tot
    sh = n_heads
    while sh < LANES:
        incl = incl + jnp.where(lane8 >= sh, pltpu.roll(incl, sh, 1), 0.0)
        gsum = gsum + pltpu.roll(gsum, sh, 1)
        sh *= 2
    gcarry = gcarry_scr[...]
    carry = (incl - tot) + gcarry
    gcarry_scr[...] = gcarry + gsum
    w = jnp.exp2(lb - between - carry[0:1, :])

    wt = w.T
    w_all = jnp.concatenate([wt[p * n_heads:(p + 1) * n_heads, :] for p in range(npp)], axis=1)
    acc_scr[...] += jnp.dot(w_all.astype(BF16), vb_scr[...], preferred_element_type=F32)

    @pl.when(g == pl.num_programs(1) - 1)
    def _():
        acc = acc_scr[...]
        hr = lax.broadcasted_iota(jnp.int32, (n_heads, HEAD_DIM), 0)
        out = jnp.zeros((n_heads, HEAD_DIM), F32)
        for hh in range(n_heads):
            out = jnp.where(hr == hh, acc[:, hh * HEAD_DIM:(hh + 1) * HEAD_DIM], out)
        o_ref[0] = out


def _sb_decode(q, cache_k, cache_v, layer, page_table, b_sb):
    dec_b, wb = q.shape
    depth, n_phys, page, n_heads, hd = cache_k.shape
    n_pages = page_table.shape[1]
    npp = PAGES_PER_STEP
    assert n_pages % npp == 0 and npp * n_heads == LANES and hd == HEAD_DIM
    ck = cache_k.reshape(depth, n_phys, page, wb)
    cv = cache_v.reshape(depth, n_phys, page, wb)
    qx = jnp.broadcast_to(q.reshape(dec_b, wb, 1), (dec_b, wb, LANES))
    bias = jnp.tile(b_sb.astype(F32), npp).reshape(1, LANES)

    def page_spec(p):
        return pl.BlockSpec((1, 1, page, wb),
                            lambda b, g, pt: (layer, pt[b, n_pages - 1 - (g * npp + p)], 0, 0))

    specs = [page_spec(p) for p in range(npp)]
    out = pl.pallas_call(
        functools.partial(_sb_decode_kernel, n_heads=n_heads, page=page),
        grid_spec=pltpu.PrefetchScalarGridSpec(
            num_scalar_prefetch=1,
            grid=(dec_b, n_pages // npp),
            in_specs=[pl.BlockSpec((1, LANES), lambda b, g, pt: (0, 0)),
                      pl.BlockSpec((1, wb, LANES), lambda b, g, pt: (b, 0, 0))] + specs + specs,
            out_specs=pl.BlockSpec((1, n_heads, HEAD_DIM), lambda b, g, pt: (b, 0, 0)),
            scratch_shapes=[pltpu.VMEM((npp * page, wb), BF16), pltpu.VMEM((npp * page, wb), BF16),
                            pltpu.VMEM((n_heads, wb), F32), pltpu.VMEM((8, LANES), F32)],
        ),
        out_shape=jax.ShapeDtypeStruct((dec_b, n_heads, HEAD_DIM), F32),
        compiler_params=_cparams("parallel", "arbitrary"),
        name="sb_decode",
    )(page_table, bias, qx, *([ck] * npp), *([cv] * npp))
    return out.reshape(dec_b, wb)


def _gate_b_kernel(b_ref, gb_ref, nbr_ref, yb_ref):
    b = b_ref[...]
    ms = jnp.mean(b * b, axis=-1, keepdims=True)
    yb = b * lax.rsqrt(ms + EPS) * nbr_ref[...] * _silu(gb_ref[...].astype(F32))
    yb_ref[...] = yb.astype(BF16)


def _gate_b(b_out, gb, norm_br_b, tm):
    m, wb = b_out.shape
    row = pl.BlockSpec((tm, wb), lambda i: (i, 0))
    return pl.pallas_call(
        _gate_b_kernel,
        grid=(m // tm,),
        in_specs=[row, row, pl.BlockSpec((1, wb), lambda i: (0, 0))],
        out_specs=row,
        out_shape=jax.ShapeDtypeStruct((m, wb), BF16),
        compiler_params=_cparams("parallel"),
        name="gate_b",
    )(b_out, gb, norm_br_b.reshape(1, wb))


def _out_proj_kernel(ya_ref, yb_ref, wa_ref, wb_ref, o_ref):
    o_ref[...] = (jnp.dot(ya_ref[...], wa_ref[...], preferred_element_type=F32)
                  + jnp.dot(yb_ref[...], wb_ref[...], preferred_element_type=F32))


def _out_proj(ya, yb, w_out_bf, tm, tn):
    m, wa = ya.shape
    wb = yb.shape[1]
    assert wa == wb
    d = w_out_bf.shape[1]
    return pl.pallas_call(
        _out_proj_kernel,
        grid=(m // tm, d // tn),
        in_specs=[pl.BlockSpec((tm, wa), lambda i, j: (i, 0)),
                  pl.BlockSpec((tm, wb), lambda i, j: (i, 0)),
                  pl.BlockSpec((wa, tn), lambda i, j: (0, j)),
                  pl.BlockSpec((wb, tn), lambda i, j: (1, j))],
        out_specs=pl.BlockSpec((tm, tn), lambda i, j: (i, j)),
        out_shape=jax.ShapeDtypeStruct((m, d), F32),
        compiler_params=_cparams("parallel", "arbitrary"),
        name="out_proj",
    )(ya, yb, w_out_bf, w_out_bf)


def _residual_kernel(x_ref, o_ref, g_ref, y_ref):
    o = o_ref[...]
    ms = jnp.mean(o * o, axis=-1, keepdims=True)
    y_ref[...] = x_ref[...] + o * lax.rsqrt(ms + EPS) * g_ref[...]


def _residual(x2d, o, norm_post, tm):
    m, d = x2d.shape
    row = pl.BlockSpec((tm, d), lambda i: (i, 0))
    return pl.pallas_call(
        _residual_kernel,
        grid=(m // tm,),
        in_specs=[row, row, pl.BlockSpec((1, d), lambda i: (0, 0))],
        out_specs=row,
        out_shape=jax.ShapeDtypeStruct((m, d), F32),
        compiler_params=_cparams("parallel"),
        name="residual",
    )(x2d, o, norm_post.reshape(1, d))


def _layer(hp, hs, cache_k, cache_v, layer, page_table, norm_pre, w_in, ln_v_g, ln_v_b, w_spatial, b_spatial, b_sb,
           norm_br_a, norm_br_b, w_out, norm_post):
    bsz, seq, d = hp.shape
    dec_b, n_new, _ = hs.shape
    assert n_new == 1
    width = w_in.shape[1] // N_SECTIONS
    n_heads = width // HEAD_DIM

    w_perm = (w_in.reshape(d, N_SECTIONS, n_heads, HEAD_DIM).transpose(0, 2, 1, 3)
              .reshape(d, n_heads * N_SECTIONS * HEAD_DIM).astype(BF16))
    w_out_bf = w_out.astype(BF16)
    causal = jnp.tril(jnp.ones((CHUNK, CHUNK), dtype=bool))
    w_tril = jnp.where(causal[None], w_spatial, 0.0).astype(BF16)
    bias_full = jnp.repeat(b_spatial.T, HEAD_DIM, axis=1)
    w0 = jnp.repeat(w_spatial[:, 0, 0], HEAD_DIM)
    b0 = jnp.repeat(b_spatial[:, 0], HEAD_DIM)

    xp = hp.reshape(bsz * seq, d)
    u, va, ga, q, k32, v32, kb, vb, gb = _in_proj(xp, norm_pre, w_perm, tm=512)
    ya, chunk_v = _branch_a(u, va, ga, ln_v_g, ln_v_b, w_tril, bias_full, norm_br_a, bsz, seq)
    b_out = _sb_prompt(q, kb, vb, b_sb, bsz, seq, tq=256)
    yb = _gate_b(b_out, gb, norm_br_b, tm=256)
    o = _out_proj(ya, yb, w_out_bf, tm=1024, tn=512)
    new_hp = _residual(xp, o, norm_post, tm=256).reshape(bsz, seq, d)

    xs = hs.reshape(dec_b, d)
    su, sva, sga, sq, sk32, sv32, _, _, sgb = _in_proj(xs, norm_pre, w_perm, tm=dec_b)
    sya, s_chunk_v = _branch_a_one(su, sva, sga, ln_v_g, ln_v_b, w0, b0, norm_br_a)
    sb_out = _sb_decode(sq, cache_k, cache_v, layer, page_table, b_sb)
    syb = _gate_b(sb_out, sgb, norm_br_b, tm=dec_b)
    so = _out_proj(sya, syb, w_out_bf, tm=dec_b, tn=512)
    new_hs = _residual(xs, so, norm_post, tm=dec_b).reshape(dec_b, 1, d)

    rows = (k32.reshape(bsz, seq, n_heads, HEAD_DIM), v32.reshape(bsz, seq, n_heads, HEAD_DIM),
            sk32.reshape(dec_b, 1, n_heads, HEAD_DIM), sv32.reshape(dec_b, 1, n_heads, HEAD_DIM),
            chunk_v, s_chunk_v.reshape(dec_b, 1, width))
    return new_hp, new_hs, rows


def kernel(x_prompt, x_sample, cache_k, cache_v, page_table, norm_pre, w_in, ln_v_g, ln_v_b, w_spatial, b_spatial,
           b_sb, norm_br_a, norm_br_b, w_out, norm_post):
    hp, hs = x_prompt, x_sample
    depth = w_in.shape[0]
    collected = [[] for _ in range(6)]
    for l in range(depth):
        hp, hs, rows = _layer(hp, hs, cache_k, cache_v, l, page_table, norm_pre[l], w_in[l], ln_v_g[l],
                              ln_v_b[l], w_spatial[l], b_spatial[l], b_sb[l], norm_br_a[l], norm_br_b[l],
                              w_out[l], norm_post[l])
        for dst, r in zip(collected, rows):
            dst.append(r)
    return (hp, hs) + tuple(jnp.stack(c) for c in collected)
```
